```python
import jax, jax.numpy as jnp
from jax import lax
import numpy as np

D_MODEL = 1024
BATCH = 2
SEQ = 16384
DEPTH = 4

N_A_LAYERS = DEPTH // 2
N_B_LAYERS = DEPTH - N_A_LAYERS
POOL_WINDOWS = (2, 4, 8, 16)
N_POOL_GROUPS = len(POOL_WINDOWS)
POOL_GROUP = D_MODEL // N_POOL_GROUPS
HEAD_DIM = 64
N_HEADS = D_MODEL // HEAD_DIM
N_KV_HEADS = 4
GQA_GROUP = N_HEADS // N_KV_HEADS
WINDOW = 128
BLOCK = 128
ROPE_THETA = 10000.0
D_FF = ((8 * D_MODEL + 3 * 256 - 1) // (3 * 256)) * 256
PLE_DIM = 256
RMS_EPS = 1e-6
NEG_INF = -1e30

kernel_name = "yoco_pool_swa_sink_hybrid"


def rms_norm(x, g):
    xf = x.astype(jnp.float32)
    y = xf * lax.rsqrt(jnp.mean(xf * xf, axis=-1, keepdims=True) + RMS_EPS)
    return (y * g.astype(jnp.float32)).astype(x.dtype)


def rope_tables(seq):
    inv = 1.0 / (ROPE_THETA ** (jnp.arange(0, HEAD_DIM, 2, dtype=jnp.float32) / HEAD_DIM))
    ang = jnp.arange(seq, dtype=jnp.float32)[:, None] * inv[None, :]
    return jnp.cos(ang), jnp.sin(ang)


def apply_rope(x, cos, sin):
    xf = x.astype(jnp.float32)
    half = HEAD_DIM // 2
    x1, x2 = xf[..., :half], xf[..., half:]
    c = cos[None, :, None, :]
    s = sin[None, :, None, :]
    return jnp.concatenate([x1 * c - x2 * s, x2 * c + x1 * s], axis=-1).astype(x.dtype)


def multiscale_pool(x, w_pool, pool_scale):
    b, s, _ = x.shape
    xf = x.astype(jnp.float32).reshape(b, s, N_POOL_GROUPS, POOL_GROUP)
    cs = jnp.cumsum(xf, axis=1)
    t = jnp.arange(s)
    pooled = []
    for g, w in enumerate(POOL_WINDOWS):
        csg = cs[:, :, g]
        lower = jnp.concatenate([jnp.zeros((b, w, POOL_GROUP), jnp.float32), csg[:, :s - w]], axis=1)
        cnt = jnp.minimum(t + 1, w).astype(jnp.float32)[None, :, None]
        pooled.append((csg - lower) / cnt - xf[:, :, g])
    pooled = jnp.stack(pooled, axis=2).astype(x.dtype)
    y = jnp.einsum('bsgc,gcd->bsgd', pooled, w_pool).reshape(b, s, D_MODEL)
    return y * pool_scale


def sliding_window_sink_attention(q, k, v, sinks):
    b, s = q.shape[0], q.shape[1]
    nb = s // BLOCK
    qb = q.reshape(b, nb, BLOCK, N_KV_HEADS, GQA_GROUP, HEAD_DIM)

    def band(t):
        tb = t.reshape(b, nb, BLOCK, N_KV_HEADS, HEAD_DIM)
        prev = jnp.concatenate([jnp.zeros_like(tb[:, :1]), tb[:, :-1]], axis=1)
        return jnp.concatenate([prev, tb], axis=2)

    kb, vb = band(k), band(v)
    scores = jnp.einsum('bnqkgd,bnskd->bnkgqs', qb, kb).astype(jnp.float32) * (HEAD_DIM ** -0.5)
    blk = jnp.arange(nb)[:, None, None] * BLOCK
    qpos = blk + jnp.arange(BLOCK)[None, :, None]
    kpos = blk - BLOCK + jnp.arange(2 * BLOCK)[None, None, :]
    diff = qpos - kpos
    valid = (diff >= 0) & (diff < WINDOW) & (kpos >= 0)
    scores = jnp.where(valid[None, :, None, None], scores, NEG_INF)
    sink = sinks.astype(jnp.float32).reshape(N_KV_HEADS, GQA_GROUP)[None, None, :, :, None, None]
    sink = jnp.broadcast_to(sink, scores.shape[:-1] + (1,))
    probs = jax.nn.softmax(jnp.concatenate([scores, sink], axis=-1), axis=-1)[..., :-1]
    out = jnp.einsum('bnkgqs,bnskd->bnqkgd', probs.astype(v.dtype), vb)
    return out.reshape(b, s, N_HEADS * HEAD_DIM)


def swiglu(x, w_gate, w_up, w_down):
    return (jax.nn.silu(x @ w_gate) * (x @ w_up)) @ w_down


def setup_inputs(seed: int = 0) -> dict:
    key = jax.random.key(seed)
    ks = jax.random.split(key, 20)
    f32 = jnp.float32

    def nrm(k, shape, fan_in):
        return jax.random.normal(k, shape, f32) * (fan_in ** -0.5)

    def gain(k, shape):
        return 1.0 + 0.05 * jax.random.normal(k, shape, f32)

    return {
        "x": jax.random.normal(ks[0], (BATCH, SEQ, D_MODEL), f32),
        "p": jax.random.normal(ks[1], (DEPTH, BATCH, SEQ, PLE_DIM), f32),
        "mix_pre_g": gain(ks[2], (DEPTH, D_MODEL)),
        "mix_post_g": gain(ks[3], (DEPTH, D_MODEL)),
        "ffn_pre_g": gain(ks[4], (DEPTH, D_MODEL)),
        "ffn_post_g": gain(ks[5], (DEPTH, D_MODEL)),
        "pool_w": nrm(ks[6], (N_A_LAYERS, N_POOL_GROUPS, POOL_GROUP, POOL_GROUP), POOL_GROUP),
        "pool_scale": 1.0 + 0.1 * jax.random.normal(ks[7], (N_A_LAYERS, D_MODEL), f32),
        "kv_norm_g": gain(ks[8], (D_MODEL,)),
        "w_k": nrm(ks[9], (D_MODEL, N_KV_HEADS * HEAD_DIM), D_MODEL),
        "w_v": nrm(ks[10], (D_MODEL, N_KV_HEADS * HEAD_DIM), D_MODEL),
        "w_q": nrm(ks[11], (N_B_LAYERS, D_MODEL, N_HEADS * HEAD_DIM), D_MODEL),
        "w_o": nrm(ks[12], (N_B_LAYERS, N_HEADS * HEAD_DIM, D_MODEL), N_HEADS * HEAD_DIM),
        "sinks": 0.5 * jax.random.normal(ks[13], (N_B_LAYERS, N_HEADS), f32),
        "w_ff_gate": nrm(ks[14], (DEPTH, D_MODEL, D_FF), D_MODEL),
        "w_ff_up": nrm(ks[15], (DEPTH, D_MODEL, D_FF), D_MODEL),
        "w_ff_down": nrm(ks[16], (DEPTH, D_FF, D_MODEL), D_FF),
        "ple_norm_g": gain(ks[17], (DEPTH, D_MODEL)),
        "w_ple_gate": nrm(ks[18], (DEPTH, D_MODEL, D_MODEL), D_MODEL),
        "w_ple_proj": nrm(ks[19], (DEPTH, PLE_DIM, D_MODEL), PLE_DIM),
    }


def reference(x, p, mix_pre_g, mix_post_g, ffn_pre_g, ffn_post_g, pool_w, pool_scale,
              kv_norm_g, w_k, w_v, w_q, w_o, sinks, w_ff_gate, w_ff_up, w_ff_down,
              ple_norm_g, w_ple_gate, w_ple_proj):
    b, s, _ = x.shape
    cos, sin = rope_tables(s)
    h = x
    k_shared = None
    v_shared = None
    for i in range(DEPTH):
        hn = rms_norm(h, mix_pre_g[i])
        if i < N_A_LAYERS:
            m = multiscale_pool(hn, pool_w[i], pool_scale[i])
        else:
            j = i - N_A_LAYERS
            q = apply_rope((hn @ w_q[j]).reshape(b, s, N_HEADS, HEAD_DIM), cos, sin)
            m = sliding_window_sink_attention(q, k_shared, v_shared, sinks[j]) @ w_o[j]
        h = h + rms_norm(m, mix_post_g[i])
        f = swiglu(rms_norm(h, ffn_pre_g[i]), w_ff_gate[i], w_ff_up[i], w_ff_down[i])
        h = h + rms_norm(f, ffn_post_g[i])
        gate = jax.nn.sigmoid(rms_norm(h, ple_norm_g[i]) @ w_ple_gate[i])
        h = h + (p[i] @ w_ple_proj[i]) * gate
        if i == N_A_LAYERS - 1:
            hk = rms_norm(h, kv_norm_g)
            k_shared = apply_rope((hk @ w_k).reshape(b, s, N_KV_HEADS, HEAD_DIM), cos, sin)
            v_shared = (hk @ w_v).reshape(b, s, N_KV_HEADS, HEAD_DIM)
    return h
```

```python
import functools

import jax
import jax.numpy as jnp
from jax import lax
from jax.experimental import pallas as pl
from jax.experimental.pallas import tpu as pltpu

D_MODEL = 1024
DEPTH = 4
N_A_LAYERS = DEPTH // 2
POOL_WINDOWS = (2, 4, 8, 16)
POOL_GROUP = D_MODEL // len(POOL_WINDOWS)
HEAD_DIM = 64
N_HEADS = D_MODEL // HEAD_DIM
N_KV_HEADS = 4
GQA_GROUP = N_HEADS // N_KV_HEADS
KV_DIM = N_KV_HEADS * HEAD_DIM
WINDOW = 128
ROPE_THETA = 10000.0
RMS_EPS = 1e-6
NEG_INF = -1e30

V7X_LANES = 128
V7X_VMEM_BYTES = 64 * 1024 * 1024

ROW_TILE = 512
POOL_HALO = max(POOL_WINDOWS)
VMEM_LIMIT_BYTES = V7X_VMEM_BYTES - 6 * 1024 * 1024

G_MIX_PRE, G_MIX_POST, G_FFN_PRE, G_FFN_POST, G_PLE, G_POOL_SCALE, G_KV = range(7)


def _rms(x, g):
    ms = jnp.mean(x * x, axis=-1, keepdims=True)
    return x * lax.rsqrt(ms + RMS_EPS) * g


def _mm(a, b):
    return jnp.dot(a, b, preferred_element_type=jnp.float32)


def _bf16(x):
    return x.astype(jnp.bfloat16)


def _rope(x, tab):
    cos = tab[:, :V7X_LANES]
    sin = tab[:, V7X_LANES:]
    lane = lax.broadcasted_iota(jnp.int32, (x.shape[0], V7X_LANES), 1)
    first_half = (lane & (HEAD_DIM // 2)) == 0
    outs = []
    for j in range(x.shape[1] // V7X_LANES):
        blk = x[:, j * V7X_LANES:(j + 1) * V7X_LANES]
        partner = jnp.where(first_half,
                            pltpu.roll(blk, V7X_LANES - HEAD_DIM // 2, 1),
                            pltpu.roll(blk, HEAD_DIM // 2, 1))
        outs.append(blk * cos + partner * sin)
    return jnp.concatenate(outs, axis=1)


def _ffn_ple(h, p_ref, g_ref, wg_ref, wu_ref, wd_ref, wpg_ref, wpp_ref):
    xn = _bf16(_rms(h, g_ref[G_FFN_PRE:G_FFN_PRE + 1, :]))
    gate = _mm(xn, wg_ref[...])
    up = _mm(xn, wu_ref[...])
    act = _bf16(gate * jax.nn.sigmoid(gate) * up)
    f = _mm(act, wd_ref[...])
    h = h + _rms(f, g_ref[G_FFN_POST:G_FFN_POST + 1, :])
    pg = jax.nn.sigmoid(_mm(_bf16(_rms(h, g_ref[G_PLE:G_PLE + 1, :])), wpg_ref[...]))
    proj = _mm(_bf16(p_ref[...]), wpp_ref[...])
    return h + proj * pg


def _pool_layer_kernel(seq, emit_kv, *refs):
    if emit_kv:
        (h_ref, halo_ref, p_ref, g_ref, pw_ref, wg_ref, wu_ref, wd_ref, wpg_ref, wpp_ref,
         wkv_ref, tab_ref, o_ref, k_ref, v_ref) = refs
    else:
        (h_ref, halo_ref, p_ref, g_ref, pw_ref, wg_ref, wu_ref, wd_ref, wpg_ref, wpp_ref,
         o_ref) = refs
    tm = h_ref.shape[0]
    pos0 = (pl.program_id(0) * tm) % seq

    h = h_ref[...]
    g_pre = g_ref[G_MIX_PRE:G_MIX_PRE + 1, :]
    hn = _rms(h, g_pre)
    halo_n = jnp.where(pos0 == 0, 0.0, _rms(halo_ref[...], g_pre))
    ext = jnp.concatenate([halo_n, hn], axis=0)

    t = pos0 + lax.broadcasted_iota(jnp.int32, (tm, 1), 0)
    mixed = []
    s = ext
    for g, w in enumerate(POOL_WINDOWS):
        s = s[:, (POOL_GROUP if g else 0):]
        s = s + pltpu.roll(s, w // 2, 0)
        cnt = jnp.minimum(t + 1, w).astype(jnp.float32)
        pooled = s[POOL_HALO:, :POOL_GROUP] / cnt - hn[:, g * POOL_GROUP:(g + 1) * POOL_GROUP]
        mixed.append(_mm(_bf16(pooled), pw_ref[g]))
    m = jnp.concatenate(mixed, axis=1) * g_ref[G_POOL_SCALE:G_POOL_SCALE + 1, :]
    h = h + _rms(m, g_ref[G_MIX_POST:G_MIX_POST + 1, :])

    h = _ffn_ple(h, p_ref, g_ref, wg_ref, wu_ref, wd_ref, wpg_ref, wpp_ref)
    o_ref[...] = h

    if emit_kv:
        hk = _bf16(_rms(h, g_ref[G_KV:G_KV + 1, :]))
        kv = _mm(hk, wkv_ref[...])
        k_ref[...] = _bf16(_rope(kv[:, :KV_DIM], tab_ref[...]))
        v_ref[...] = _bf16(kv[:, KV_DIM:])


def _attn_layer_kernel(seq, sink_ref, h_ref, p_ref, g_ref, wq_ref, wo_ref, k_ref, khalo_ref,
                       v_ref, vhalo_ref, tab_ref, wg_ref, wu_ref, wd_ref, wpg_ref, wpp_ref,
                       o_ref, attn_ref):
    tm = h_ref.shape[0]
    pos0 = (pl.program_id(0) * tm) % seq

    h = h_ref[...]
    hn = _bf16(_rms(h, g_ref[G_MIX_PRE:G_MIX_PRE + 1, :]))
    q = _bf16(_rope(_mm(hn, wq_ref[...]), tab_ref[...]))

    qi = lax.broadcasted_iota(jnp.int32, (WINDOW, 2 * WINDOW), 0)
    kj = lax.broadcasted_iota(jnp.int32, (WINDOW, 2 * WINDOW), 1)
    band = (kj > qi) & (kj <= qi + WINDOW)

    for b in range(tm // WINDOW):
        r0 = b * WINDOW
        if b == 0:
            kb = jnp.concatenate([khalo_ref[...], k_ref[0:WINDOW, :]], axis=0)
            vb = jnp.concatenate([vhalo_ref[...], v_ref[0:WINDOW, :]], axis=0)
            valid = band & ((kj >= WINDOW) | (pos0 > 0))
        else:
            kb = k_ref[r0 - WINDOW:r0 + WINDOW, :]
            vb = v_ref[r0 - WINDOW:r0 + WINDOW, :]
            valid = band
        valid = jnp.concatenate([valid] * GQA_GROUP, axis=0)
        qb = q[r0:r0 + WINDOW, :]
        for kh in range(N_KV_HEADS):
            heads = [kh * GQA_GROUP + g for g in range(GQA_GROUP)]
            qs = jnp.concatenate([qb[:, hd * HEAD_DIM:(hd + 1) * HEAD_DIM] for hd in heads], axis=0)
            ks = kb[:, kh * HEAD_DIM:(kh + 1) * HEAD_DIM]
            vs = vb[:, kh * HEAD_DIM:(kh + 1) * HEAD_DIM]
            s = lax.dot_general(qs, ks, (((1,), (1,)), ((), ())),
                                preferred_element_type=jnp.float32)
            s = jnp.where(valid, s, NEG_INF)
            sink = jnp.concatenate(
                [jnp.full((WINDOW, 1), sink_ref[hd], jnp.float32) for hd in heads], axis=0)
            mx = jnp.maximum(jnp.max(s, axis=-1, keepdims=True), sink)
            e = jnp.exp(s - mx)
            den = jnp.sum(e, axis=-1, keepdims=True) + jnp.exp(sink - mx)
            o = _mm(_bf16(e), vs) / den
            for g, hd in enumerate(heads):
                attn_ref[r0:r0 + WINDOW, hd * HEAD_DIM:(hd + 1) * HEAD_DIM] = (
                    o[g * WINDOW:(g + 1) * WINDOW, :])

    m = _mm(_bf16(attn_ref[...]), wo_ref[...])
    h = h + _rms(m, g_ref[G_MIX_POST:G_MIX_POST + 1, :])
    o_ref[...] = _ffn_ple(h, p_ref, g_ref, wg_ref, wu_ref, wd_ref, wpg_ref, wpp_ref)


def _resident(shape, index):
    return pl.BlockSpec(shape, lambda t: index, pipeline_mode=pl.Buffered(1))


def _layer_weight_specs(layer, d_ff, ple_dim):
    return [
        _resident((None, D_MODEL, d_ff), (layer, 0, 0)),
        _resident((None, D_MODEL, d_ff), (layer, 0, 0)),
        _resident((None, d_ff, D_MODEL), (layer, 0, 0)),
        _resident((None, D_MODEL, D_MODEL), (layer, 0, 0)),
        _resident((None, ple_dim, D_MODEL), (layer, 0, 0)),
    ]


def _compiler_params():
    return pltpu.CompilerParams(dimension_semantics=("arbitrary",),
                                vmem_limit_bytes=VMEM_LIMIT_BYTES)


def _pool_layer(layer, h, p, gains, pool_w, ffn_w, kv=None, *, seq):
    n = h.shape[0]
    tm = ROW_TILE
    d_ff = ffn_w[0].shape[-1]
    ple_dim = p.shape[-1]
    emit_kv = kv is not None
    halo_blocks = tm // POOL_HALO
    in_specs = [
        pl.BlockSpec((tm, D_MODEL), lambda t: (t, 0)),
        pl.BlockSpec((POOL_HALO, D_MODEL), lambda t: (jnp.maximum(t * halo_blocks - 1, 0), 0)),
        pl.BlockSpec((None, tm, ple_dim), lambda t: (layer, t, 0)),
        _resident((None, 8, D_MODEL), (layer, 0, 0)),
        _resident((None, len(POOL_WINDOWS), POOL_GROUP, POOL_GROUP), (layer, 0, 0, 0)),
    ] + _layer_weight_specs(layer, d_ff, ple_dim)
    args = [h, h, p, gains, pool_w, *ffn_w]
    out_shape = [jax.ShapeDtypeStruct((n, D_MODEL), jnp.float32)]
    out_specs = [pl.BlockSpec((tm, D_MODEL), lambda t: (t, 0))]
    if emit_kv:
        w_kv, k_tab = kv
        in_specs += [
            _resident((D_MODEL, 2 * KV_DIM), (0, 0)),
            pl.BlockSpec((tm, 2 * V7X_LANES), lambda t: (t % (seq // tm), 0)),
        ]
        args += [w_kv, k_tab]
        out_shape += [jax.ShapeDtypeStruct((n, KV_DIM), jnp.bfloat16)] * 2
        out_specs += [pl.BlockSpec((tm, KV_DIM), lambda t: (t, 0))] * 2
    return pl.pallas_call(
        functools.partial(_pool_layer_kernel, seq, emit_kv),
        grid=(n // tm,),
        in_specs=in_specs,
        out_specs=out_specs,
        out_shape=out_shape,
        compiler_params=_compiler_params(),
        name=f"pool_layer_{layer}",
    )(*args)


def _attn_layer(layer, h, p, gains, sinks, w_q, w_o, k, v, q_tab, ffn_w, *, seq):
    n = h.shape[0]
    tm = ROW_TILE
    d_ff = ffn_w[0].shape[-1]
    ple_dim = p.shape[-1]
    j = layer - N_A_LAYERS
    halo_blocks = tm // WINDOW
    main_kv = pl.BlockSpec((tm, KV_DIM), lambda t: (t, 0))
    halo_kv = pl.BlockSpec((WINDOW, KV_DIM), lambda t: (jnp.maximum(t * halo_blocks - 1, 0), 0))
    in_specs = [
        pl.BlockSpec(memory_space=pltpu.SMEM),
        pl.BlockSpec((tm, D_MODEL), lambda t: (t, 0)),
        pl.BlockSpec((None, tm, ple_dim), lambda t: (layer, t, 0)),
        _resident((None, 8, D_MODEL), (layer, 0, 0)),
        _resident((None, D_MODEL, D_MODEL), (j, 0, 0)),
        _resident((None, D_MODEL, D_MODEL), (j, 0, 0)),
        main_kv, halo_kv, main_kv, halo_kv,
        pl.BlockSpec((tm, 2 * V7X_LANES), lambda t: (t % (seq // tm), 0)),
    ] + _layer_weight_specs(layer, d_ff, ple_dim)
    return pl.pallas_call(
        functools.partial(_attn_layer_kernel, seq),
        grid=(n // tm,),
        in_specs=in_specs,
        out_specs=pl.BlockSpec((tm, D_MODEL), lambda t: (t, 0)),
        out_shape=jax.ShapeDtypeStruct((n, D_MODEL), jnp.float32),
        scratch_shapes=[pltpu.VMEM((tm, D_MODEL), jnp.float32)],
        compiler_params=_compiler_params(),
        name=f"attn_layer_{layer}",
    )(sinks[j], h, p, gains, w_q, w_o, k, k, v, v, q_tab, *ffn_w)


def _rope_table(seq, scale):
    inv = 1.0 / (ROPE_THETA ** (jnp.arange(0, HEAD_DIM, 2, dtype=jnp.float32) / HEAD_DIM))
    ang = jnp.arange(seq, dtype=jnp.float32)[:, None] * inv[None, :]
    cos, sin = jnp.cos(ang), jnp.sin(ang)
    reps = V7X_LANES // HEAD_DIM
    cos_t = jnp.tile(jnp.concatenate([cos, cos], axis=1), (1, reps))
    sin_t = jnp.tile(jnp.concatenate([-sin, sin], axis=1), (1, reps))
    return jnp.concatenate([cos_t, sin_t], axis=1) * scale


def kernel(x, p, mix_pre_g, mix_post_g, ffn_pre_g, ffn_post_g, pool_w, pool_scale, kv_norm_g, w_k, w_v, w_q, w_o, sinks, w_ff_gate, w_ff_up, w_ff_down, ple_norm_g, w_ple_gate, w_ple_proj):
    b, s, d = x.shape
    n = b * s
    assert d == D_MODEL and s % ROW_TILE == 0 and ROW_TILE % WINDOW == 0
    depth = p.shape[0]
    n_a = pool_w.shape[0]

    pad_scale = jnp.concatenate([pool_scale, jnp.ones((depth - n_a, d), jnp.float32)], axis=0)
    zeros = jnp.zeros((depth, d), jnp.float32)
    gains = jnp.stack([mix_pre_g, mix_post_g, ffn_pre_g, ffn_post_g, ple_norm_g, pad_scale,
                       jnp.broadcast_to(kv_norm_g, (depth, d)), zeros], axis=1)

    ffn_w = tuple(w.astype(jnp.bfloat16) for w in (w_ff_gate, w_ff_up, w_ff_down, w_ple_gate, w_ple_proj))
    pool_w16 = pool_w.astype(jnp.bfloat16)
    w_kv = jnp.concatenate([w_k, w_v], axis=1).astype(jnp.bfloat16)
    w_q16 = w_q.astype(jnp.bfloat16)
    w_o16 = w_o.astype(jnp.bfloat16)
    k_tab = _rope_table(s, 1.0)
    q_tab = _rope_table(s, HEAD_DIM ** -0.5)

    h = x.reshape(n, d)
    pf = p.reshape(depth, n, p.shape[-1])
    k = v = None
    for i in range(depth):
        if i < n_a:
            if i == n_a - 1:
                h, k, v = _pool_layer(i, h, pf, gains, pool_w16, ffn_w, (w_kv, k_tab), seq=s)
            else:
                (h,) = _pool_layer(i, h, pf, gains, pool_w16, ffn_w, seq=s)
        else:
            h = _attn_layer(i, h, pf, gains, sinks, w_q16, w_o16, k, v, q_tab, ffn_w, seq=s)
    return h.reshape(b, s, d)
```

```python
import functools

import jax
import jax.numpy as jnp
from jax import lax
from jax.experimental import pallas as pl
from jax.experimental.pallas import tpu as pltpu

D_MODEL = 1024
DEPTH = 4
N_A_LAYERS = DEPTH // 2
POOL_WINDOWS = (2, 4, 8, 16)
POOL_GROUP = D_MODEL // len(POOL_WINDOWS)
HEAD_DIM = 64
HALF_HEAD = HEAD_DIM // 2
N_HEADS = D_MODEL // HEAD_DIM
N_KV_HEADS = 4
GQA_GROUP = N_HEADS // N_KV_HEADS
KV_DIM = N_KV_HEADS * HEAD_DIM
WINDOW = 128
ROPE_THETA = 10000.0
RMS_EPS = 1e-6
NEG_INF = -1e30

V7X_LANES = 128
V7X_VMEM_BYTES = 64 * 1024 * 1024

ROW_TILE = 512
POOL_HALO = max(POOL_WINDOWS)
VMEM_LIMIT_BYTES = V7X_VMEM_BYTES - 6 * 1024 * 1024

G_MIX_PRE, G_MIX_POST, G_FFN_PRE, G_FFN_POST, G_PLE, G_POOL_SCALE, G_KV = range(7)


def _rms(x, g):
    ms = jnp.mean(x * x, axis=-1, keepdims=True)
    return x * lax.rsqrt(ms + RMS_EPS) * g


def _mm(a, b):
    return jnp.dot(a, b, preferred_element_type=jnp.float32)


def _mm_nt(a, b):
    return lax.dot_general(a, b, (((1,), (1,)), ((), ())), preferred_element_type=jnp.float32)


def _bf16(x):
    return x.astype(jnp.bfloat16)


def _rope(x, tab):
    cos = tab[:, :V7X_LANES]
    sin = tab[:, V7X_LANES:]
    lane = lax.broadcasted_iota(jnp.int32, (x.shape[0], V7X_LANES), 1)
    first_half = (lane & HALF_HEAD) == 0
    outs = []
    for j in range(x.shape[1] // V7X_LANES):
        blk = x[:, j * V7X_LANES:(j + 1) * V7X_LANES]
        partner = jnp.where(first_half,
                            pltpu.roll(blk, V7X_LANES - HALF_HEAD, 1),
                            pltpu.roll(blk, HALF_HEAD, 1))
        outs.append(blk * cos + partner * sin)
    return jnp.concatenate(outs, axis=1)


def _rope_t(xt, tab_t):
    cos = tab_t[:HALF_HEAD, :]
    sin = tab_t[HALF_HEAD:, :]
    pieces = []
    for hd in range(xt.shape[0] // HEAD_DIM):
        x1 = xt[hd * HEAD_DIM:hd * HEAD_DIM + HALF_HEAD, :]
        x2 = xt[hd * HEAD_DIM + HALF_HEAD:(hd + 1) * HEAD_DIM, :]
        pieces += [x1 * cos - x2 * sin, x2 * cos + x1 * sin]
    return jnp.concatenate(pieces, axis=0)


def _ffn_ple(h, p_ref, g_ref, wg_ref, wu_ref, wd_ref, wpg_ref, wpp_ref):
    xn = _bf16(_rms(h, g_ref[G_FFN_PRE:G_FFN_PRE + 1, :]))
    gate = _mm(xn, wg_ref[...])
    up = _mm(xn, wu_ref[...])
    act = _bf16(gate * jax.nn.sigmoid(gate) * up)
    f = _mm(act, wd_ref[...])
    h = h + _rms(f, g_ref[G_FFN_POST:G_FFN_POST + 1, :])
    pg = jax.nn.sigmoid(_mm(_bf16(_rms(h, g_ref[G_PLE:G_PLE + 1, :])), wpg_ref[...]))
    proj = _mm(_bf16(p_ref[...]), wpp_ref[...])
    return h + proj * pg


def _pool_layer_kernel(seq, emit_kv, *refs):
    if emit_kv:
        (h_ref, halo_ref, p_ref, g_ref, pw_ref, wg_ref, wu_ref, wd_ref, wpg_ref, wpp_ref,
         wk_ref, wvt_ref, tab_ref, o_ref, k_ref, vt_ref) = refs
    else:
        (h_ref, halo_ref, p_ref, g_ref, pw_ref, wg_ref, wu_ref, wd_ref, wpg_ref, wpp_ref,
         o_ref) = refs
    tm = h_ref.shape[0]
    pos0 = (pl.program_id(0) * tm) % seq

    h = h_ref[...]
    g_pre = g_ref[G_MIX_PRE:G_MIX_PRE + 1, :]
    hn = _rms(h, g_pre)
    halo_n = jnp.where(pos0 == 0, 0.0, _rms(halo_ref[...], g_pre))
    ext = jnp.concatenate([halo_n, hn], axis=0)

    t = pos0 + lax.broadcasted_iota(jnp.int32, (tm, 1), 0)
    mixed = []
    s = ext
    for g, w in enumerate(POOL_WINDOWS):
        s = s[:, (POOL_GROUP if g else 0):]
        s = s + pltpu.roll(s, w // 2, 0)
        cnt = jnp.minimum(t + 1, w).astype(jnp.float32)
        pooled = s[POOL_HALO:, :POOL_GROUP] / cnt - hn[:, g * POOL_GROUP:(g + 1) * POOL_GROUP]
        mixed.append(_mm(_bf16(pooled), pw_ref[g]))
    m = jnp.concatenate(mixed, axis=1) * g_ref[G_POOL_SCALE:G_POOL_SCALE + 1, :]
    h = h + _rms(m, g_ref[G_MIX_POST:G_MIX_POST + 1, :])

    h = _ffn_ple(h, p_ref, g_ref, wg_ref, wu_ref, wd_ref, wpg_ref, wpp_ref)
    o_ref[...] = h

    if emit_kv:
        hk = _bf16(_rms(h, g_ref[G_KV:G_KV + 1, :]))
        k_ref[...] = _bf16(_rope(_mm(hk, wk_ref[...]), tab_ref[...]))
        vt_ref[...] = _bf16(_mm_nt(wvt_ref[...], hk))


def _attn_layer_kernel(seq, sink_ref, h_ref, p_ref, g_ref, wqt_ref, wot_ref, k_ref, khalo_ref,
                       vt_ref, vthalo_ref, tab_ref, wg_ref, wu_ref, wd_ref, wpg_ref, wpp_ref,
                       o_ref, attn_t_ref):
    tm = h_ref.shape[0]
    pos0 = (pl.program_id(0) * tm) % seq

    h = h_ref[...]
    hn = _bf16(_rms(h, g_ref[G_MIX_PRE:G_MIX_PRE + 1, :]))
    q_t = _bf16(_rope_t(_mm_nt(wqt_ref[...], hn), tab_ref[...]))

    kj = lax.broadcasted_iota(jnp.int32, (2 * WINDOW, GQA_GROUP * WINDOW), 0)
    qi = lax.broadcasted_iota(jnp.int32, (2 * WINDOW, GQA_GROUP * WINDOW), 1) & (WINDOW - 1)
    band = (kj > qi) & (kj <= qi + WINDOW)
    zero_half = jnp.zeros((HEAD_DIM, WINDOW), jnp.bfloat16)

    for b in range(tm // WINDOW):
        r0 = b * WINDOW
        if b == 0:
            kb = jnp.concatenate([khalo_ref[...], k_ref[0:WINDOW, :]], axis=0)
            vtb = jnp.concatenate([vthalo_ref[...], vt_ref[:, 0:WINDOW]], axis=1)
            valid = band & ((kj >= WINDOW) | (pos0 > 0))
        else:
            kb = k_ref[r0 - WINDOW:r0 + WINDOW, :]
            vtb = vt_ref[:, r0 - WINDOW:r0 + WINDOW]
            valid = band
        for kh in range(N_KV_HEADS):
            heads = [kh * GQA_GROUP + g for g in range(GQA_GROUP)]
            pair = (kh // 2) * 2 * HEAD_DIM
            k_pair = kb[:, pair:pair + 2 * HEAD_DIM]
            cols = []
            for hd in heads:
                q_h = q_t[hd * HEAD_DIM:(hd + 1) * HEAD_DIM, r0:r0 + WINDOW]
                cols.append(jnp.concatenate([q_h, zero_half] if kh % 2 == 0 else [zero_half, q_h],
                                            axis=0))
            s = _mm(k_pair, jnp.concatenate(cols, axis=1))
            s = jnp.where(valid, s, NEG_INF)
            sink = jnp.concatenate(
                [jnp.full((1, WINDOW), sink_ref[hd], jnp.float32) for hd in heads], axis=1)
            mx = jnp.maximum(jnp.max(s, axis=0, keepdims=True), sink)
            e = jnp.exp(s - mx)
            den = jnp.sum(e, axis=0, keepdims=True) + jnp.exp(sink - mx)
            o_t = _mm(vtb[kh * HEAD_DIM:(kh + 1) * HEAD_DIM, :], _bf16(e)) * (1.0 / den)
            for g, hd in enumerate(heads):
                attn_t_ref[hd * HEAD_DIM:(hd + 1) * HEAD_DIM, r0:r0 + WINDOW] = _bf16(
                    o_t[:, g * WINDOW:(g + 1) * WINDOW])

    m = _mm(wot_ref[...], attn_t_ref[...]).T
    h = h + _rms(m, g_ref[G_MIX_POST:G_MIX_POST + 1, :])
    o_ref[...] = _ffn_ple(h, p_ref, g_ref, wg_ref, wu_ref, wd_ref, wpg_ref, wpp_ref)


def _resident(shape, index):
    return pl.BlockSpec(shape, lambda t: index, pipeline_mode=pl.Buffered(1))


def _layer_weight_specs(layer, d_ff, ple_dim):
    return [
        _resident((None, D_MODEL, d_ff), (layer, 0, 0)),
        _resident((None, D_MODEL, d_ff), (layer, 0, 0)),
        _resident((None, d_ff, D_MODEL), (layer, 0, 0)),
        _resident((None, D_MODEL, D_MODEL), (layer, 0, 0)),
        _resident((None, ple_dim, D_MODEL), (layer, 0, 0)),
    ]


def _compiler_params():
    return pltpu.CompilerParams(dimension_semantics=("arbitrary",),
                                vmem_limit_bytes=VMEM_LIMIT_BYTES)


def _pool_layer(layer, h, p, gains, pool_w, ffn_w, kv=None, *, seq):
    n = h.shape[0]
    tm = ROW_TILE
    d_ff = ffn_w[0].shape[-1]
    ple_dim = p.shape[-1]
    emit_kv = kv is not None
    halo_blocks = tm // POOL_HALO
    in_specs = [
        pl.BlockSpec((tm, D_MODEL), lambda t: (t, 0)),
        pl.BlockSpec((POOL_HALO, D_MODEL), lambda t: (jnp.maximum(t * halo_blocks - 1, 0), 0)),
        pl.BlockSpec((None, tm, ple_dim), lambda t: (layer, t, 0)),
        _resident((None, 8, D_MODEL), (layer, 0, 0)),
        _resident((None, len(POOL_WINDOWS), POOL_GROUP, POOL_GROUP), (layer, 0, 0, 0)),
    ] + _layer_weight_specs(layer, d_ff, ple_dim)
    args = [h, h, p, gains, pool_w, *ffn_w]
    out_shape = [jax.ShapeDtypeStruct((n, D_MODEL), jnp.float32)]
    out_specs = [pl.BlockSpec((tm, D_MODEL), lambda t: (t, 0))]
    if emit_kv:
        w_k, w_vt, k_tab = kv
        in_specs += [
            _resident((D_MODEL, KV_DIM), (0, 0)),
            _resident((KV_DIM, D_MODEL), (0, 0)),
            pl.BlockSpec((tm, 2 * V7X_LANES), lambda t: (t % (seq // tm), 0)),
        ]
        args += [w_k, w_vt, k_tab]
        out_shape += [jax.ShapeDtypeStruct((n, KV_DIM), jnp.bfloat16),
                      jax.ShapeDtypeStruct((KV_DIM, n), jnp.bfloat16)]
        out_specs += [pl.BlockSpec((tm, KV_DIM), lambda t: (t, 0)),
                      pl.BlockSpec((KV_DIM, tm), lambda t: (0, t))]
    return pl.pallas_call(
        functools.partial(_pool_layer_kernel, seq, emit_kv),
        grid=(n // tm,),
        in_specs=in_specs,
        out_specs=out_specs,
        out_shape=out_shape,
        compiler_params=_compiler_params(),
        name=f"pool_layer_{layer}",
    )(*args)


def _attn_layer(layer, h, p, gains, sinks, w_qt, w_ot, k, v_t, q_tab_t, ffn_w, *, seq):
    n = h.shape[0]
    tm = ROW_TILE
    d_ff = ffn_w[0].shape[-1]
    ple_dim = p.shape[-1]
    j = layer - N_A_LAYERS
    halo_blocks = tm // WINDOW
    in_specs = [
        pl.BlockSpec(memory_space=pltpu.SMEM),
        pl.BlockSpec((tm, D_MODEL), lambda t: (t, 0)),
        pl.BlockSpec((None, tm, ple_dim), lambda t: (layer, t, 0)),
        _resident((None, 8, D_MODEL), (layer, 0, 0)),
        _resident((None, D_MODEL, D_MODEL), (j, 0, 0)),
        _resident((None, D_MODEL, D_MODEL), (j, 0, 0)),
        pl.BlockSpec((tm, KV_DIM), lambda t: (t, 0)),
        pl.BlockSpec((WINDOW, KV_DIM), lambda t: (jnp.maximum(t * halo_blocks - 1, 0), 0)),
        pl.BlockSpec((KV_DIM, tm), lambda t: (0, t)),
        pl.BlockSpec((KV_DIM, WINDOW), lambda t: (0, jnp.maximum(t * halo_blocks - 1, 0))),
        pl.BlockSpec((HEAD_DIM, tm), lambda t: (0, t % (seq // tm))),
    ] + _layer_weight_specs(layer, d_ff, ple_dim)
    return pl.pallas_call(
        functools.partial(_attn_layer_kernel, seq),
        grid=(n // tm,),
        in_specs=in_specs,
        out_specs=pl.BlockSpec((tm, D_MODEL), lambda t: (t, 0)),
        out_shape=jax.ShapeDtypeStruct((n, D_MODEL), jnp.float32),
        scratch_shapes=[pltpu.VMEM((D_MODEL, tm), jnp.bfloat16)],
        compiler_params=_compiler_params(),
        name=f"attn_layer_{layer}",
    )(sinks[j], h, p, gains, w_qt, w_ot, k, k, v_t, v_t, q_tab_t, *ffn_w)


def _rope_angles(seq):
    inv = 1.0 / (ROPE_THETA ** (jnp.arange(0, HEAD_DIM, 2, dtype=jnp.float32) / HEAD_DIM))
    ang = jnp.arange(seq, dtype=jnp.float32)[:, None] * inv[None, :]
    return jnp.cos(ang), jnp.sin(ang)


def _rope_table(seq):
    cos, sin = _rope_angles(seq)
    reps = V7X_LANES // HEAD_DIM
    cos_t = jnp.tile(jnp.concatenate([cos, cos], axis=1), (1, reps))
    sin_t = jnp.tile(jnp.concatenate([-sin, sin], axis=1), (1, reps))
    return jnp.concatenate([cos_t, sin_t], axis=1)


def _rope_table_t(seq, scale):
    cos, sin = _rope_angles(seq)
    return jnp.concatenate([cos.T, sin.T], axis=0) * scale


def kernel(x, p, mix_pre_g, mix_post_g, ffn_pre_g, ffn_post_g, pool_w, pool_scale, kv_norm_g, w_k, w_v, w_q, w_o, sinks, w_ff_gate, w_ff_up, w_ff_down, ple_norm_g, w_ple_gate, w_ple_proj):
    b, s, d = x.shape
    n = b * s
    assert d == D_MODEL and s % ROW_TILE == 0 and ROW_TILE % WINDOW == 0
    depth = p.shape[0]
    n_a = pool_w.shape[0]

    pad_scale = jnp.concatenate([pool_scale, jnp.ones((depth - n_a, d), jnp.float32)], axis=0)
    zeros = jnp.zeros((depth, d), jnp.float32)
    gains = jnp.stack([mix_pre_g, mix_post_g, ffn_pre_g, ffn_post_g, ple_norm_g, pad_scale,
                       jnp.broadcast_to(kv_norm_g, (depth, d)), zeros], axis=1)

    ffn_w = tuple(w.astype(jnp.bfloat16) for w in (w_ff_gate, w_ff_up, w_ff_down, w_ple_gate, w_ple_proj))
    pool_w16 = pool_w.astype(jnp.bfloat16)
    w_k16 = w_k.astype(jnp.bfloat16)
    w_vt16 = w_v.T.astype(jnp.bfloat16)
    w_qt16 = jnp.swapaxes(w_q, 1, 2).astype(jnp.bfloat16)
    w_ot16 = jnp.swapaxes(w_o, 1, 2).astype(jnp.bfloat16)
    k_tab = _rope_table(s)
    q_tab_t = _rope_table_t(s, HEAD_DIM ** -0.5)

    h = x.reshape(n, d)
    pf = p.reshape(depth, n, p.shape[-1])
    k = v_t = None
    for i in range(depth):
        if i < n_a:
            if i == n_a - 1:
                h, k, v_t = _pool_layer(i, h, pf, gains, pool_w16, ffn_w, (w_k16, w_vt16, k_tab), seq=s)
            else:
                (h,) = _pool_layer(i, h, pf, gains, pool_w16, ffn_w, seq=s)
        else:
            h = _attn_layer(i, h, pf, gains, sinks, w_qt16, w_ot16, k, v_t, q_tab_t, ffn_w, seq=s)
    return h.reshape(b, s, d)
```

```python
import functools
import math

import jax
import jax.numpy as jnp
from jax import lax
from jax.experimental import pallas as pl
from jax.experimental.pallas import tpu as pltpu

D_MODEL = 1024
DEPTH = 4
N_A_LAYERS = DEPTH // 2
POOL_WINDOWS = (2, 4, 8, 16)
POOL_GROUP = D_MODEL // len(POOL_WINDOWS)
HEAD_DIM = 64
HALF_HEAD = HEAD_DIM // 2
N_HEADS = D_MODEL // HEAD_DIM
N_KV_HEADS = 4
GQA_GROUP = N_HEADS // N_KV_HEADS
KV_DIM = N_KV_HEADS * HEAD_DIM
WINDOW = 128
ROPE_THETA = 10000.0
RMS_EPS = 1e-6
NEG_INF = -1e30
LOG2_E = math.log2(math.e)

V7X_LANES = 128
BF16_SUBLANES = 16
V7X_MXU_DIM = 256
V7X_VMEM_BYTES = 64 * 1024 * 1024

ROW_TILE = 512
POOL_HALO = max(POOL_WINDOWS)
SCORE_LOOKAHEAD = 2
FF_CHUNK = V7X_MXU_DIM
DOWN_CHUNK = 2 * V7X_MXU_DIM
DOWN_ROWS = 256
VMEM_LIMIT_BYTES = V7X_VMEM_BYTES - 6 * 1024 * 1024

G_MIX_PRE, G_MIX_POST, G_FFN_PRE, G_FFN_POST, G_PLE, G_POOL_SCALE, G_KV = range(7)


def _gain(g_ref, row):
    return g_ref[row:row + 1, :]


def _rms(x, g):
    ms = jnp.mean(x * x, axis=-1, keepdims=True)
    return x * lax.rsqrt(ms + RMS_EPS) * g


def _mm(a, b):
    return jnp.dot(a, b, preferred_element_type=jnp.float32)


def _mm_nt(a, b):
    return lax.dot_general(a, b, (((1,), (1,)), ((), ())), preferred_element_type=jnp.float32)


def _bf16(x):
    return x.astype(jnp.bfloat16)


def _rope(x, tab):
    cos = tab[:, :V7X_LANES]
    sin = tab[:, V7X_LANES:]
    lane = lax.broadcasted_iota(jnp.int32, (x.shape[0], V7X_LANES), 1)
    first_half = (lane & HALF_HEAD) == 0
    outs = []
    for j in range(x.shape[1] // V7X_LANES):
        blk = x[:, j * V7X_LANES:(j + 1) * V7X_LANES]
        partner = jnp.where(first_half,
                            pltpu.roll(blk, V7X_LANES - HALF_HEAD, 1),
                            pltpu.roll(blk, HALF_HEAD, 1))
        outs.append(blk * cos + partner * sin)
    return jnp.concatenate(outs, axis=1)


def _rope_t(xt, tab_t):
    cos = tab_t[:HALF_HEAD, :]
    sin = tab_t[HALF_HEAD:, :]
    pieces = []
    for hd in range(xt.shape[0] // HEAD_DIM):
        x1 = xt[hd * HEAD_DIM:hd * HEAD_DIM + HALF_HEAD, :]
        x2 = xt[hd * HEAD_DIM + HALF_HEAD:(hd + 1) * HEAD_DIM, :]
        pieces += [x1 * cos - x2 * sin, x2 * cos + x1 * sin]
    return jnp.concatenate(pieces, axis=0)


def _carry_out(h1, g_ref, h1_ref, xn_ref):
    h1_ref[...] = h1
    xn_ref[...] = _bf16(_rms(h1, _gain(g_ref, G_FFN_PRE)))


def _back_stages(h1_ref, xn_ref, act_ref, p_ref, g_ref, wgu_ref, wd_ref, wpg_ref, wpp_ref, finish):
    d_ff = wd_ref.shape[0]
    proj = _mm(_bf16(p_ref[...]), wpp_ref[...])
    yield
    for j in range(d_ff // FF_CHUNK):
        gu = _mm(xn_ref[...], wgu_ref[:, 2 * FF_CHUNK * j:2 * FF_CHUNK * (j + 1)])
        gate, up = gu[:, :FF_CHUNK], gu[:, FF_CHUNK:]
        act_ref[:, FF_CHUNK * j:FF_CHUNK * (j + 1)] = _bf16(gate * jax.nn.sigmoid(gate) * up)
        yield
    tm = act_ref.shape[0]
    row_blocks = []
    for r in range(tm // DOWN_ROWS):
        pieces = []
        for n in range(D_MODEL // DOWN_CHUNK):
            pieces.append(_mm(act_ref[DOWN_ROWS * r:DOWN_ROWS * (r + 1), :],
                              wd_ref[:, DOWN_CHUNK * n:DOWN_CHUNK * (n + 1)]))
            yield
        row_blocks.append(jnp.concatenate(pieces, axis=1))
    h = h1_ref[...] + _rms(jnp.concatenate(row_blocks, axis=0), _gain(g_ref, G_FFN_POST))
    pg = _mm(_bf16(_rms(h, _gain(g_ref, G_PLE))), wpg_ref[...])
    yield
    finish(h + proj * jax.nn.sigmoid(pg))
    yield


def _pool_front_stages(pos0, h_ref, halo_ref, g_ref, pw_ref, h1_ref, xn_ref):
    tm = h_ref.shape[0]
    h = h_ref[...]
    g_pre = _gain(g_ref, G_MIX_PRE)
    hn = _rms(h, g_pre)
    halo_n = jnp.where(pos0 == 0, 0.0, _rms(halo_ref[...], g_pre))
    ext = jnp.concatenate([halo_n, hn], axis=0)

    t = pos0 + lax.broadcasted_iota(jnp.int32, (tm, 1), 0)
    mixed = []
    s = ext
    for g, w in enumerate(POOL_WINDOWS):
        s = s[:, (POOL_GROUP if g else 0):]
        s = s + pltpu.roll(s, w // 2, 0)
        cnt = jnp.minimum(t + 1, w).astype(jnp.float32)
        pooled = s[POOL_HALO:, :POOL_GROUP] / cnt - hn[:, g * POOL_GROUP:(g + 1) * POOL_GROUP]
        mixed.append(_mm(_bf16(pooled), pw_ref[g]))
    yield
    m = jnp.concatenate(mixed, axis=1) * _gain(g_ref, G_POOL_SCALE)
    _carry_out(h + _rms(m, _gain(g_ref, G_MIX_POST)), g_ref, h1_ref, xn_ref)
    yield


def _attn_front_stages(pos0, sink_ref, h_ref, g_ref, wqt_ref, wot_ref, k_ref, khalo_ref, vt_ref,
                       vthalo_ref, tab_ref, attn_t_ref, h1_ref, xn_ref):
    tm = h_ref.shape[0]
    hn = _bf16(_rms(h_ref[...], _gain(g_ref, G_MIX_PRE)))
    q_lin = _mm_nt(wqt_ref[...], hn)
    yield
    q_t = _bf16(_rope_t(q_lin, tab_ref[...]))

    kj = lax.broadcasted_iota(jnp.int32, (2 * WINDOW, GQA_GROUP * WINDOW), 0)
    qi = lax.broadcasted_iota(jnp.int32, (2 * WINDOW, GQA_GROUP * WINDOW), 1) & (WINDOW - 1)
    band = (kj > qi) & (kj <= qi + WINDOW)
    zero_half = jnp.zeros((HEAD_DIM, WINDOW), jnp.bfloat16)
    ones_rows = jnp.ones((BF16_SUBLANES, 2 * WINDOW), jnp.bfloat16)

    def block_operands(b):
        r0 = b * WINDOW
        if b == 0:
            kb = jnp.concatenate([khalo_ref[...], k_ref[0:WINDOW, :]], axis=0)
            vtb = jnp.concatenate([vthalo_ref[...], vt_ref[:, 0:WINDOW]], axis=1)
            valid = band & ((kj >= WINDOW) | (pos0 > 0))
        else:
            kb = k_ref[r0 - WINDOW:r0 + WINDOW, :]
            vtb = vt_ref[:, r0 - WINDOW:r0 + WINDOW]
            valid = band
        return kb, vtb, valid

    def scores(b, kh):
        kb, _, _ = block_operands(b)
        r0 = b * WINDOW
        pair = (kh // 2) * 2 * HEAD_DIM
        cols = []
        for g in range(GQA_GROUP):
            hd = kh * GQA_GROUP + g
            q_h = q_t[hd * HEAD_DIM:(hd + 1) * HEAD_DIM, r0:r0 + WINDOW]
            cols.append(jnp.concatenate([q_h, zero_half] if kh % 2 == 0 else [zero_half, q_h], axis=0))
        return _mm(kb[:, pair:pair + 2 * HEAD_DIM], jnp.concatenate(cols, axis=1))

    def attend(b, kh, s):
        _, vtb, valid = block_operands(b)
        r0 = b * WINDOW
        heads = [kh * GQA_GROUP + g for g in range(GQA_GROUP)]
        s = jnp.where(valid, s, NEG_INF)
        sink = jnp.concatenate(
            [jnp.full((1, WINDOW), sink_ref[hd] * LOG2_E, jnp.float32) for hd in heads], axis=1)
        mx = jnp.maximum(jnp.max(s, axis=0, keepdims=True), sink)
        e = _bf16(jnp.exp2(s - mx))
        pv = _mm(jnp.concatenate([vtb[kh * HEAD_DIM:(kh + 1) * HEAD_DIM, :], ones_rows], axis=0), e)
        den = pv[HEAD_DIM:HEAD_DIM + 1, :] + jnp.exp2(sink - mx)
        o_t = pv[:HEAD_DIM, :] * (1.0 / den)
        for g, hd in enumerate(heads):
            attn_t_ref[hd * HEAD_DIM:(hd + 1) * HEAD_DIM, r0:r0 + WINDOW] = _bf16(
                o_t[:, g * WINDOW:(g + 1) * WINDOW])

    units = [(b, kh) for b in range(tm // WINDOW) for kh in range(N_KV_HEADS)]
    pending = [scores(*u) for u in units[:SCORE_LOOKAHEAD]]
    yield
    for i, unit in enumerate(units):
        if i + SCORE_LOOKAHEAD < len(units):
            pending.append(scores(*units[i + SCORE_LOOKAHEAD]))
        attend(*unit, pending.pop(0))
        yield

    m_t = _mm(wot_ref[...], attn_t_ref[...])
    yield
    h1 = h_ref[...] + _rms(m_t.T, _gain(g_ref, G_MIX_POST))
    _carry_out(h1, g_ref, h1_ref, xn_ref)
    yield


def _n_back_pieces(tm, d_ff):
    return d_ff // FF_CHUNK + (tm // DOWN_ROWS) * (D_MODEL // DOWN_CHUNK)


def _run(gen, n=1):
    for _ in range(n):
        next(gen)


def _front_tile(n_tiles):
    return jnp.minimum(pl.program_id(0), n_tiles - 1)


def _init_carry(h1_ref, xn_ref):
    @pl.when(pl.program_id(0) == 0)
    def _():
        h1_ref[...] = jnp.zeros_like(h1_ref)
        xn_ref[...] = jnp.zeros_like(xn_ref)


def _pool_layer_kernel(seq, n_tiles, emit_kv, *refs):
    if emit_kv:
        (h_ref, halo_ref, p_ref, g_ref, pw_ref, wgu_ref, wd_ref, wpg_ref, wpp_ref,
         wk_ref, wvt_ref, tab_ref, o_ref, k_ref, vt_ref, h1_ref, xn_ref, act_ref) = refs
    else:
        (h_ref, halo_ref, p_ref, g_ref, pw_ref, wgu_ref, wd_ref, wpg_ref, wpp_ref,
         o_ref, h1_ref, xn_ref, act_ref) = refs
    tm = h_ref.shape[0]
    pos0 = (_front_tile(n_tiles) * tm) % seq
    _init_carry(h1_ref, xn_ref)

    def finish(h):
        o_ref[...] = h
        if emit_kv:
            hk = _bf16(_rms(h, _gain(g_ref, G_KV)))
            k_ref[...] = _bf16(_rope(_mm(hk, wk_ref[...]), tab_ref[...]))
            vt_ref[...] = _bf16(_mm_nt(wvt_ref[...], hk))

    back = _back_stages(h1_ref, xn_ref, act_ref, p_ref, g_ref, wgu_ref, wd_ref, wpg_ref, wpp_ref,
                        finish)
    front = _pool_front_stages(pos0, h_ref, halo_ref, g_ref, pw_ref, h1_ref, xn_ref)
    n_pieces = _n_back_pieces(tm, wd_ref.shape[0])
    _run(back, 2)
    _run(front)
    _run(back, n_pieces - 1)
    _run(back)
    _run(front)
    _run(back)


def _attn_layer_kernel(seq, n_tiles, sink_ref, h_ref, p_ref, g_ref, wqt_ref, wot_ref, k_ref,
                       khalo_ref, vt_ref, vthalo_ref, tab_ref, wgu_ref, wd_ref, wpg_ref, wpp_ref,
                       o_ref, attn_t_ref, h1_ref, xn_ref, act_ref):
    tm = h_ref.shape[0]
    pos0 = (_front_tile(n_tiles) * tm) % seq
    _init_carry(h1_ref, xn_ref)

    def finish(h):
        o_ref[...] = h

    back = _back_stages(h1_ref, xn_ref, act_ref, p_ref, g_ref, wgu_ref, wd_ref, wpg_ref, wpp_ref,
                        finish)
    front = _attn_front_stages(pos0, sink_ref, h_ref, g_ref, wqt_ref, wot_ref, k_ref, khalo_ref,
                               vt_ref, vthalo_ref, tab_ref, attn_t_ref, h1_ref, xn_ref)
    n_units = (tm // WINDOW) * N_KV_HEADS
    n_pieces = _n_back_pieces(tm, wd_ref.shape[0])
    _run(back, 2)
    _run(front)
    _run(back)
    _run(front)
    lead = max(n_units - (n_pieces - 3), 0)
    _run(front, lead)
    for i in range(n_units - lead):
        _run(back)
        _run(front)
    _run(back, n_pieces - 2 - (n_units - lead))
    _run(front)
    _run(back)
    _run(front)
    _run(back)


def _resident(shape, index):
    return pl.BlockSpec(shape, lambda t: index, pipeline_mode=pl.Buffered(1))


def _layer_weight_specs(layer, d_ff, ple_dim):
    return [
        _resident((None, D_MODEL, 2 * d_ff), (layer, 0, 0)),
        _resident((None, d_ff, D_MODEL), (layer, 0, 0)),
        _resident((None, D_MODEL, D_MODEL), (layer, 0, 0)),
        _resident((None, ple_dim, D_MODEL), (layer, 0, 0)),
    ]


def _carry_scratch(tm, d_ff):
    return [pltpu.VMEM((tm, D_MODEL), jnp.float32),
            pltpu.VMEM((tm, D_MODEL), jnp.bfloat16),
            pltpu.VMEM((tm, d_ff), jnp.bfloat16)]


def _compiler_params():
    return pltpu.CompilerParams(dimension_semantics=("arbitrary",),
                                vmem_limit_bytes=VMEM_LIMIT_BYTES)


def _pool_layer(layer, h, p, gains, pool_w, ffn_w, kv=None, *, seq):
    n = h.shape[0]
    tm = ROW_TILE
    n_tiles = n // tm
    d_ff = ffn_w[1].shape[-2]
    ple_dim = p.shape[-1]
    emit_kv = kv is not None
    halo_blocks = tm // POOL_HALO

    def front(t):
        return jnp.minimum(t, n_tiles - 1)

    def back(t):
        return jnp.maximum(t - 1, 0)

    in_specs = [
        pl.BlockSpec((tm, D_MODEL), lambda t: (front(t), 0)),
        pl.BlockSpec((POOL_HALO, D_MODEL), lambda t: (jnp.maximum(front(t) * halo_blocks - 1, 0), 0)),
        pl.BlockSpec((None, tm, ple_dim), lambda t: (layer, back(t), 0)),
        _resident((None, 8, D_MODEL), (layer, 0, 0)),
        _resident((None, len(POOL_WINDOWS), POOL_GROUP, POOL_GROUP), (layer, 0, 0, 0)),
    ] + _layer_weight_specs(layer, d_ff, ple_dim)
    args = [h, h, p, gains, pool_w, *ffn_w]
    out_shape = [jax.ShapeDtypeStruct((n, D_MODEL), jnp.float32)]
    out_specs = [pl.BlockSpec((tm, D_MODEL), lambda t: (back(t), 0))]
    if emit_kv:
        w_k, w_vt, k_tab = kv
        in_specs += [
            _resident((D_MODEL, KV_DIM), (0, 0)),
            _resident((KV_DIM, D_MODEL), (0, 0)),
            pl.BlockSpec((tm, 2 * V7X_LANES), lambda t: (back(t) % (seq // tm), 0)),
        ]
        args += [w_k, w_vt, k_tab]
        out_shape += [jax.ShapeDtypeStruct((n, KV_DIM), jnp.bfloat16),
                      jax.ShapeDtypeStruct((KV_DIM, n), jnp.bfloat16)]
        out_specs += [pl.BlockSpec((tm, KV_DIM), lambda t: (back(t), 0)),
                      pl.BlockSpec((KV_DIM, tm), lambda t: (0, back(t)))]
    return pl.pallas_call(
        functools.partial(_pool_layer_kernel, seq, n_tiles, emit_kv),
        grid=(n_tiles + 1,),
        in_specs=in_specs,
        out_specs=out_specs,
        out_shape=out_shape,
        scratch_shapes=_carry_scratch(tm, d_ff),
        compiler_params=_compiler_params(),
        name=f"pool_layer_{layer}",
    )(*args)


def _attn_layer(layer, h, p, gains, sinks, w_qt, w_ot, k, v_t, q_tab_t, ffn_w, *, seq):
    n = h.shape[0]
    tm = ROW_TILE
    n_tiles = n // tm
    d_ff = ffn_w[1].shape[-2]
    ple_dim = p.shape[-1]
    j = layer - N_A_LAYERS
    halo_blocks = tm // WINDOW

    def front(t):
        return jnp.minimum(t, n_tiles - 1)

    def back(t):
        return jnp.maximum(t - 1, 0)

    def halo(t):
        return jnp.maximum(front(t) * halo_blocks - 1, 0)

    in_specs = [
        pl.BlockSpec(memory_space=pltpu.SMEM),
        pl.BlockSpec((tm, D_MODEL), lambda t: (front(t), 0)),
        pl.BlockSpec((None, tm, ple_dim), lambda t: (layer, back(t), 0)),
        _resident((None, 8, D_MODEL), (layer, 0, 0)),
        _resident((None, D_MODEL, D_MODEL), (j, 0, 0)),
        _resident((None, D_MODEL, D_MODEL), (j, 0, 0)),
        pl.BlockSpec((tm, KV_DIM), lambda t: (front(t), 0)),
        pl.BlockSpec((WINDOW, KV_DIM), lambda t: (halo(t), 0)),
        pl.BlockSpec((KV_DIM, tm), lambda t: (0, front(t))),
        pl.BlockSpec((KV_DIM, WINDOW), lambda t: (0, halo(t))),
        pl.BlockSpec((HEAD_DIM, tm), lambda t: (0, front(t) % (seq // tm))),
    ] + _layer_weight_specs(layer, d_ff, ple_dim)
    return pl.pallas_call(
        functools.partial(_attn_layer_kernel, seq, n_tiles),
        grid=(n_tiles + 1,),
        in_specs=in_specs,
        out_specs=pl.BlockSpec((tm, D_MODEL), lambda t: (back(t), 0)),
        out_shape=jax.ShapeDtypeStruct((n, D_MODEL), jnp.float32),
        scratch_shapes=[pltpu.VMEM((D_MODEL, tm), jnp.bfloat16)] + _carry_scratch(tm, d_ff),
        compiler_params=_compiler_params(),
        name=f"attn_layer_{layer}",
    )(sinks[j], h, p, gains, w_qt, w_ot, k, k, v_t, v_t, q_tab_t, *ffn_w)


def _rope_angles(seq):
    inv = 1.0 / (ROPE_THETA ** (jnp.arange(0, HEAD_DIM, 2, dtype=jnp.float32) / HEAD_DIM))
    ang = jnp.arange(seq, dtype=jnp.float32)[:, None] * inv[None, :]
    return jnp.cos(ang), jnp.sin(ang)


def _rope_table(seq):
    cos, sin = _rope_angles(seq)
    reps = V7X_LANES // HEAD_DIM
    cos_t = jnp.tile(jnp.concatenate([cos, cos], axis=1), (1, reps))
    sin_t = jnp.tile(jnp.concatenate([-sin, sin], axis=1), (1, reps))
    return jnp.concatenate([cos_t, sin_t], axis=1)


def _rope_table_t(seq, scale):
    cos, sin = _rope_angles(seq)
    return jnp.concatenate([cos.T, sin.T], axis=0) * scale


def _gate_up_chunks(w_gate, w_up):
    depth, d, d_ff = w_gate.shape
    pairs = jnp.stack([w_gate.reshape(depth, d, d_ff // FF_CHUNK, FF_CHUNK),
                       w_up.reshape(depth, d, d_ff // FF_CHUNK, FF_CHUNK)], axis=3)
    return pairs.reshape(depth, d, 2 * d_ff).astype(jnp.bfloat16)


def kernel(x, p, mix_pre_g, mix_post_g, ffn_pre_g, ffn_post_g, pool_w, pool_scale, kv_norm_g, w_k, w_v, w_q, w_o, sinks, w_ff_gate, w_ff_up, w_ff_down, ple_norm_g, w_ple_gate, w_ple_proj):
    b, s, d = x.shape
    n = b * s
    d_ff = w_ff_gate.shape[-1]
    assert d == D_MODEL and s % ROW_TILE == 0 and ROW_TILE % WINDOW == 0 and d_ff % FF_CHUNK == 0
    depth = p.shape[0]
    n_a = pool_w.shape[0]

    pad_scale = jnp.concatenate([pool_scale, jnp.ones((depth - n_a, d), jnp.float32)], axis=0)
    zeros = jnp.zeros((depth, d), jnp.float32)
    gains = jnp.stack([mix_pre_g, mix_post_g, ffn_pre_g, ffn_post_g, ple_norm_g, pad_scale,
                       jnp.broadcast_to(kv_norm_g, (depth, d)), zeros], axis=1)

    ffn_w = (_gate_up_chunks(w_ff_gate, w_ff_up), w_ff_down.astype(jnp.bfloat16),
             w_ple_gate.astype(jnp.bfloat16), w_ple_proj.astype(jnp.bfloat16))
    pool_w16 = pool_w.astype(jnp.bfloat16)
    w_k16 = w_k.astype(jnp.bfloat16)
    w_vt16 = w_v.T.astype(jnp.bfloat16)
    w_qt16 = jnp.swapaxes(w_q, 1, 2).astype(jnp.bfloat16)
    w_ot16 = jnp.swapaxes(w_o, 1, 2).astype(jnp.bfloat16)
    k_tab = _rope_table(s)
    q_tab_t = _rope_table_t(s, HEAD_DIM ** -0.5 * LOG2_E)

    h = x.reshape(n, d)
    pf = p.reshape(depth, n, p.shape[-1])
    k = v_t = None
    for i in range(depth):
        if i < n_a:
            if i == n_a - 1:
                h, k, v_t = _pool_layer(i, h, pf, gains, pool_w16, ffn_w, (w_k16, w_vt16, k_tab), seq=s)
            else:
                (h,) = _pool_layer(i, h, pf, gains, pool_w16, ffn_w, seq=s)
        else:
            h = _attn_layer(i, h, pf, gains, sinks, w_qt16, w_ot16, k, v_t, q_tab_t, ffn_w, seq=s)
    return h.reshape(b, s, d)
```

```python
import functools
import math

import jax
import jax.numpy as jnp
from jax import lax
from jax.experimental import pallas as pl
from jax.experimental.pallas import tpu as pltpu

D_MODEL = 1024
DEPTH = 4
N_A_LAYERS = DEPTH // 2
POOL_WINDOWS = (2, 4, 8, 16)
POOL_GROUP = D_MODEL // len(POOL_WINDOWS)
HEAD_DIM = 64
HALF_HEAD = HEAD_DIM // 2
N_HEADS = D_MODEL // HEAD_DIM
N_KV_HEADS = 4
GQA_GROUP = N_HEADS // N_KV_HEADS
KV_DIM = N_KV_HEADS * HEAD_DIM
WINDOW = 128
ROPE_THETA = 10000.0
RMS_EPS = 1e-6
NEG_INF = -1e30
LOG2_E = math.log2(math.e)

V7X_LANES = 128
BF16_SUBLANES = 16
V7X_MXU_DIM = 256
V7X_VMEM_BYTES = 64 * 1024 * 1024

ROW_TILE = 512
HALF_ROWS = ROW_TILE // 2
POOL_ROWS = 128
POOL_HALO = max(POOL_WINDOWS)
POOL_BLOCK_STAGES = 5
SCORE_LOOKAHEAD = 2
FF_CHUNK = V7X_MXU_DIM
DOWN_CHUNK = 2 * V7X_MXU_DIM
VMEM_LIMIT_BYTES = V7X_VMEM_BYTES - 6 * 1024 * 1024

G_MIX_PRE, G_MIX_POST, G_FFN_PRE, G_FFN_POST, G_PLE, G_POOL_SCALE, G_KV = range(7)


def _gain(g_ref, row):
    return g_ref[row:row + 1, :]


def _rms(x, g):
    ms = jnp.mean(x * x, axis=-1, keepdims=True)
    return x * lax.rsqrt(ms + RMS_EPS) * g


def _mm(a, b):
    return jnp.dot(a, b, preferred_element_type=jnp.float32)


def _mm_nt(a, b):
    return lax.dot_general(a, b, (((1,), (1,)), ((), ())), preferred_element_type=jnp.float32)


def _bf16(x):
    return x.astype(jnp.bfloat16)


def _halves(tm):
    return [slice(r * HALF_ROWS, (r + 1) * HALF_ROWS) for r in range(tm // HALF_ROWS)]


def _rope_t(xt, tab_t):
    cos = tab_t[:HALF_HEAD, :]
    sin = tab_t[HALF_HEAD:, :]
    pieces = []
    for hd in range(xt.shape[0] // HEAD_DIM):
        x1 = xt[hd * HEAD_DIM:hd * HEAD_DIM + HALF_HEAD, :]
        x2 = xt[hd * HEAD_DIM + HALF_HEAD:(hd + 1) * HEAD_DIM, :]
        pieces += [x1 * cos - x2 * sin, x2 * cos + x1 * sin]
    return jnp.concatenate(pieces, axis=0)


def _carry_out(rows, h1, g_ref, h1_ref, xn_ref, slot):
    h1_ref[slot, rows, :] = h1
    xn_ref[slot, rows, :] = _bf16(_rms(h1, _gain(g_ref, G_FFN_PRE)))


def _back_stages(slot, h1_ref, xn_ref, act_ref, p_ref, g_ref, wg_ref, wu_ref, wd_ref, wpg_ref,
                 wpp_ref, finish):
    tm, d_ff = act_ref.shape
    for j in range(d_ff // FF_CHUNK):
        cols = slice(FF_CHUNK * j, FF_CHUNK * (j + 1))
        gu = _mm(xn_ref[slot], jnp.concatenate([wg_ref[:, cols], wu_ref[:, cols]], axis=1))
        gate, up = gu[:, :FF_CHUNK], gu[:, FF_CHUNK:]
        act_ref[:, cols] = _bf16(gate * jax.nn.sigmoid(gate) * up)
        yield
    normed, gated = [], []

    def gate_stage(r):
        rows = _halves(tm)[r]
        gated.append((_mm(normed[r][1], wpg_ref[...]), _mm(_bf16(p_ref[rows, :]), wpp_ref[...])))
        yield

    for r, rows in enumerate(_halves(tm)):
        pieces = []
        for n in range(D_MODEL // DOWN_CHUNK):
            pieces.append(_mm(act_ref[rows, :], wd_ref[:, DOWN_CHUNK * n:DOWN_CHUNK * (n + 1)]))
            yield
        if r > 0:
            yield from gate_stage(r - 1)
        h = h1_ref[slot, rows, :] + _rms(jnp.concatenate(pieces, axis=1), _gain(g_ref, G_FFN_POST))
        normed.append((h, _bf16(_rms(h, _gain(g_ref, G_PLE)))))
        yield
    yield from gate_stage(len(normed) - 1)
    for rows, (h, _), (pg, proj) in zip(_halves(tm), normed, gated):
        finish(rows, h + proj * jax.nn.sigmoid(pg))
        yield


def _n_back_pieces(tm, d_ff):
    return d_ff // FF_CHUNK + len(_halves(tm)) * (D_MODEL // DOWN_CHUNK)


def _pool_block_stages(pos0, rows, slot, h_ref, halo_ref, g_ref, pw_ref, h1_ref, xn_ref, bands):
    g_pre = _gain(g_ref, G_MIX_PRE)
    if rows.start == 0:
        ext = jnp.concatenate([jnp.where(pos0 == 0, 0.0, _rms(halo_ref[...], g_pre)),
                               _rms(h_ref[rows, :], g_pre)], axis=0)
    else:
        ext = _rms(h_ref[rows.start - POOL_HALO:rows.stop, :], g_pre)
    ext16 = _bf16(ext)
    hn = ext16[POOL_HALO:, :].astype(jnp.float32)
    yield
    sums = [_mm(band, ext16[:, g * POOL_GROUP:(g + 1) * POOL_GROUP]) for g, band in enumerate(bands)]
    yield
    t = pos0 + rows.start + lax.broadcasted_iota(jnp.int32, (POOL_ROWS, 1), 0)
    pooled = []
    for g, w in enumerate(POOL_WINDOWS):
        inv_cnt = 1.0 / jnp.minimum(t + 1, w).astype(jnp.float32)
        pooled.append(_bf16(sums[g] * inv_cnt - hn[:, g * POOL_GROUP:(g + 1) * POOL_GROUP]))
    yield
    m = jnp.concatenate([_mm(x, pw_ref[g]) for g, x in enumerate(pooled)], axis=1)
    yield
    h1 = h_ref[rows, :] + _rms(m * _gain(g_ref, G_POOL_SCALE), _gain(g_ref, G_MIX_POST))
    _carry_out(rows, h1, g_ref, h1_ref, xn_ref, slot)
    yield


def _pool_bands():
    i = lax.broadcasted_iota(jnp.int32, (POOL_ROWS, POOL_HALO + POOL_ROWS), 0)
    j = lax.broadcasted_iota(jnp.int32, (POOL_ROWS, POOL_HALO + POOL_ROWS), 1) - POOL_HALO
    return [jnp.where((j <= i) & (j > i - w), 1.0, 0.0).astype(jnp.bfloat16) for w in POOL_WINDOWS]


def _attn_front_stages(pos0, slot, sink_ref, h_ref, g_ref, wqt_ref, wot_ref, k_ref, khalo_ref,
                       vt_ref, vthalo_ref, tab_ref, attn_t_ref, h1_ref, xn_ref):
    tm = h_ref.shape[0]
    hn = _bf16(_rms(h_ref[...], _gain(g_ref, G_MIX_PRE)))
    yield
    q_lin = _mm_nt(wqt_ref[...], hn)
    yield
    q_t = _bf16(_rope_t(q_lin, tab_ref[...]))

    kj = lax.broadcasted_iota(jnp.int32, (2 * WINDOW, GQA_GROUP * WINDOW), 0)
    qi = lax.broadcasted_iota(jnp.int32, (2 * WINDOW, GQA_GROUP * WINDOW), 1) & (WINDOW - 1)
    band = (kj > qi) & (kj <= qi + WINDOW)
    zero_half = jnp.zeros((HEAD_DIM, WINDOW), jnp.bfloat16)
    ones_rows = jnp.ones((BF16_SUBLANES, 2 * WINDOW), jnp.bfloat16)

    def block_operands(b):
        r0 = b * WINDOW
        if b == 0:
            kb = jnp.concatenate([khalo_ref[...], k_ref[0:WINDOW, :]], axis=0)
            vtb = jnp.concatenate([vthalo_ref[...], vt_ref[:, 0:WINDOW]], axis=1)
            valid = band & ((kj >= WINDOW) | (pos0 > 0))
        else:
            kb = k_ref[r0 - WINDOW:r0 + WINDOW, :]
            vtb = vt_ref[:, r0 - WINDOW:r0 + WINDOW]
            valid = band
        return kb, vtb, valid

    def scores(b, kh):
        kb, _, _ = block_operands(b)
        r0 = b * WINDOW
        pair = (kh // 2) * 2 * HEAD_DIM
        cols = []
        for g in range(GQA_GROUP):
            hd = kh * GQA_GROUP + g
            q_h = q_t[hd * HEAD_DIM:(hd + 1) * HEAD_DIM, r0:r0 + WINDOW]
            cols.append(jnp.concatenate([q_h, zero_half] if kh % 2 == 0 else [zero_half, q_h], axis=0))
        return _mm(kb[:, pair:pair + 2 * HEAD_DIM], jnp.concatenate(cols, axis=1))

    def attend(b, kh, s):
        _, vtb, valid = block_operands(b)
        r0 = b * WINDOW
        heads = [kh * GQA_GROUP + g for g in range(GQA_GROUP)]
        s = jnp.where(valid, s, NEG_INF)
        sink = jnp.concatenate(
            [jnp.full((1, WINDOW), sink_ref[hd] * LOG2_E, jnp.float32) for hd in heads], axis=1)
        mx = jnp.maximum(jnp.max(s, axis=0, keepdims=True), sink)
        e = _bf16(jnp.exp2(s - mx))
        pv = _mm(jnp.concatenate([vtb[kh * HEAD_DIM:(kh + 1) * HEAD_DIM, :], ones_rows], axis=0), e)
        den = pv[HEAD_DIM:HEAD_DIM + 1, :] + jnp.exp2(sink - mx)
        o_t = pv[:HEAD_DIM, :] * (1.0 / den)
        for g, hd in enumerate(heads):
            attn_t_ref[hd * HEAD_DIM:(hd + 1) * HEAD_DIM, r0:r0 + WINDOW] = _bf16(
                o_t[:, g * WINDOW:(g + 1) * WINDOW])

    units = [(b, kh) for b in range(tm // WINDOW) for kh in range(N_KV_HEADS)]
    pending = [scores(*u) for u in units[:SCORE_LOOKAHEAD]]
    yield
    for i, unit in enumerate(units):
        if i + SCORE_LOOKAHEAD < len(units):
            pending.append(scores(*units[i + SCORE_LOOKAHEAD]))
        attend(*unit, pending.pop(0))
        yield

    mixed_t = []
    for rows in _halves(tm):
        mixed_t.append(_mm(wot_ref[...], attn_t_ref[:, rows]))
        yield
    for rows, m_t in zip(_halves(tm), mixed_t):
        h1 = h_ref[rows, :] + _rms(m_t.T, _gain(g_ref, G_MIX_POST))
        _carry_out(rows, h1, g_ref, h1_ref, xn_ref, slot)
        yield


def _run(gen, n=1):
    for _ in range(n):
        next(gen)


def _finish(front, back):
    assert next(front, None) is None
    for _ in back:
        pass


def _front_tile(n_tiles):
    return jnp.minimum(pl.program_id(0), n_tiles - 1)


def _carry_slots(h1_ref, xn_ref):
    step = pl.program_id(0)

    @pl.when(step == 0)
    def _():
        h1_ref[1] = jnp.zeros(h1_ref.shape[1:], h1_ref.dtype)
        xn_ref[1] = jnp.zeros(xn_ref.shape[1:], xn_ref.dtype)

    return step % 2, (step + 1) % 2


def _pool_layer_kernel(seq, n_tiles, emit_kv, *refs):
    if emit_kv:
        (h_ref, halo_ref, p_ref, g_ref, pw_ref, wg_ref, wu_ref, wd_ref, wpg_ref, wpp_ref,
         wkvt_ref, tab_ref, o_ref, k_ref, vt_ref, h1_ref, xn_ref, act_ref) = refs
    else:
        (h_ref, halo_ref, p_ref, g_ref, pw_ref, wg_ref, wu_ref, wd_ref, wpg_ref, wpp_ref,
         o_ref, h1_ref, xn_ref, act_ref) = refs
    tm = h_ref.shape[0]
    pos0 = (_front_tile(n_tiles) * tm) % seq
    front_slot, back_slot = _carry_slots(h1_ref, xn_ref)

    def finish(rows, h):
        o_ref[rows, :] = h
        if emit_kv:
            hk = _bf16(_rms(h, _gain(g_ref, G_KV)))
            kv_t = _mm_nt(wkvt_ref[...], hk)
            k_ref[rows, :] = _bf16(_rope_t(kv_t[:KV_DIM, :], tab_ref[:, rows]).T)
            vt_ref[:, rows] = _bf16(kv_t[KV_DIM:, :])

    back = _back_stages(back_slot, h1_ref, xn_ref, act_ref, p_ref, g_ref, wg_ref, wu_ref, wd_ref,
                        wpg_ref, wpp_ref, finish)
    bands = _pool_bands()
    blocks = [_pool_block_stages(pos0, slice(r0, r0 + POOL_ROWS), front_slot, h_ref, halo_ref, g_ref,
                                 pw_ref, h1_ref, xn_ref, bands) for r0 in range(0, tm, POOL_ROWS)]
    n_slots = len(blocks) + POOL_BLOCK_STAGES - 1
    assert wd_ref.shape[0] // FF_CHUNK >= n_slots
    for k in range(n_slots):
        _run(back)
        for i, blk in enumerate(blocks):
            if 0 <= k - i < POOL_BLOCK_STAGES:
                _run(blk)
    for blk in blocks:
        assert next(blk, None) is None
    for _ in back:
        pass


def _attn_layer_kernel(seq, n_tiles, sink_ref, h_ref, p_ref, g_ref, wqt_ref, wot_ref, k_ref,
                       khalo_ref, vt_ref, vthalo_ref, tab_ref, wg_ref, wu_ref, wd_ref, wpg_ref,
                       wpp_ref, o_ref, attn_t_ref, h1_ref, xn_ref, act_ref):
    tm = h_ref.shape[0]
    pos0 = (_front_tile(n_tiles) * tm) % seq
    front_slot, back_slot = _carry_slots(h1_ref, xn_ref)

    def finish(rows, h):
        o_ref[rows, :] = h

    back = _back_stages(back_slot, h1_ref, xn_ref, act_ref, p_ref, g_ref, wg_ref, wu_ref, wd_ref,
                        wpg_ref, wpp_ref, finish)
    front = _attn_front_stages(pos0, front_slot, sink_ref, h_ref, g_ref, wqt_ref, wot_ref, k_ref,
                               khalo_ref, vt_ref, vthalo_ref, tab_ref, attn_t_ref, h1_ref, xn_ref)
    n_units = (tm // WINDOW) * N_KV_HEADS
    n_gate = wd_ref.shape[0] // FF_CHUNK
    assert len(_halves(tm)) == 2 and D_MODEL // DOWN_CHUNK == 2
    for _ in range(3):
        _run(back)
        _run(front)
    per_piece = n_units // (n_gate - 3)
    assert per_piece * (n_gate - 3) == n_units
    for _ in range(n_gate - 3):
        _run(back)
        _run(front, per_piece)
    _run(front)
    _run(back)
    _run(front, 2)
    _run(back)
    _run(front)
    _finish(front, back)


def _resident(shape, index):
    return pl.BlockSpec(shape, lambda t: index, pipeline_mode=pl.Buffered(1))


def _layer_weight_specs(layer, d_ff, ple_dim):
    return [
        _resident((None, D_MODEL, d_ff), (layer, 0, 0)),
        _resident((None, D_MODEL, d_ff), (layer, 0, 0)),
        _resident((None, d_ff, D_MODEL), (layer, 0, 0)),
        _resident((None, D_MODEL, D_MODEL), (layer, 0, 0)),
        _resident((None, ple_dim, D_MODEL), (layer, 0, 0)),
    ]


def _carry_scratch(tm, d_ff):
    return [pltpu.VMEM((2, tm, D_MODEL), jnp.float32),
            pltpu.VMEM((2, tm, D_MODEL), jnp.bfloat16),
            pltpu.VMEM((tm, d_ff), jnp.bfloat16)]


def _compiler_params():
    return pltpu.CompilerParams(dimension_semantics=("arbitrary",),
                                vmem_limit_bytes=VMEM_LIMIT_BYTES)


def _pool_layer(layer, h, p, gains, pool_w, ffn_w, kv=None, *, seq):
    n = h.shape[0]
    tm = ROW_TILE
    n_tiles = n // tm
    d_ff = ffn_w[0].shape[-1]
    ple_dim = p.shape[-1]
    emit_kv = kv is not None
    halo_blocks = tm // POOL_HALO

    def front(t):
        return jnp.minimum(t, n_tiles - 1)

    def back(t):
        return jnp.maximum(t - 1, 0)

    in_specs = [
        pl.BlockSpec((tm, D_MODEL), lambda t: (front(t), 0)),
        pl.BlockSpec((POOL_HALO, D_MODEL), lambda t: (jnp.maximum(front(t) * halo_blocks - 1, 0), 0)),
        pl.BlockSpec((None, tm, ple_dim), lambda t: (layer, back(t), 0)),
        _resident((None, 8, D_MODEL), (layer, 0, 0)),
        _resident((None, len(POOL_WINDOWS), POOL_GROUP, POOL_GROUP), (layer, 0, 0, 0)),
    ] + _layer_weight_specs(layer, d_ff, ple_dim)
    args = [h, h, p, gains, pool_w, *ffn_w]
    out_shape = [jax.ShapeDtypeStruct((n, D_MODEL), jnp.float32)]
    out_specs = [pl.BlockSpec((tm, D_MODEL), lambda t: (back(t), 0))]
    if emit_kv:
        w_kvt, k_tab_t = kv
        in_specs += [
            _resident((2 * KV_DIM, D_MODEL), (0, 0)),
            pl.BlockSpec((HEAD_DIM, tm), lambda t: (0, back(t) % (seq // tm))),
        ]
        args += [w_kvt, k_tab_t]
        out_shape += [jax.ShapeDtypeStruct((n, KV_DIM), jnp.bfloat16),
                      jax.ShapeDtypeStruct((KV_DIM, n), jnp.bfloat16)]
        out_specs += [pl.BlockSpec((tm, KV_DIM), lambda t: (back(t), 0)),
                      pl.BlockSpec((KV_DIM, tm), lambda t: (0, back(t)))]
    return pl.pallas_call(
        functools.partial(_pool_layer_kernel, seq, n_tiles, emit_kv),
        grid=(n_tiles + 1,),
        in_specs=in_specs,
        out_specs=out_specs,
        out_shape=out_shape,
        scratch_shapes=_carry_scratch(tm, d_ff),
        compiler_params=_compiler_params(),
        name=f"pool_layer_{layer}",
    )(*args)


def _attn_layer(layer, h, p, gains, sinks, w_qt, w_ot, k, v_t, q_tab_t, ffn_w, *, seq):
    n = h.shape[0]
    tm = ROW_TILE
    n_tiles = n // tm
    d_ff = ffn_w[0].shape[-1]
    ple_dim = p.shape[-1]
    j = layer - N_A_LAYERS
    halo_blocks = tm // WINDOW

    def front(t):
        return jnp.minimum(t, n_tiles - 1)

    def back(t):
        return jnp.maximum(t - 1, 0)

    def halo(t):
        return jnp.maximum(front(t) * halo_blocks - 1, 0)

    in_specs = [
        pl.BlockSpec(memory_space=pltpu.SMEM),
        pl.BlockSpec((tm, D_MODEL), lambda t: (front(t), 0)),
        pl.BlockSpec((None, tm, ple_dim), lambda t: (layer, back(t), 0)),
        _resident((None, 8, D_MODEL), (layer, 0, 0)),
        _resident((None, D_MODEL, D_MODEL), (j, 0, 0)),
        _resident((None, D_MODEL, D_MODEL), (j, 0, 0)),
        pl.BlockSpec((tm, KV_DIM), lambda t: (front(t), 0)),
        pl.BlockSpec((WINDOW, KV_DIM), lambda t: (halo(t), 0)),
        pl.BlockSpec((KV_DIM, tm), lambda t: (0, front(t))),
        pl.BlockSpec((KV_DIM, WINDOW), lambda t: (0, halo(t))),
        pl.BlockSpec((HEAD_DIM, tm), lambda t: (0, front(t) % (seq // tm))),
    ] + _layer_weight_specs(layer, d_ff, ple_dim)
    return pl.pallas_call(
        functools.partial(_attn_layer_kernel, seq, n_tiles),
        grid=(n_tiles + 1,),
        in_specs=in_specs,
        out_specs=pl.BlockSpec((tm, D_MODEL), lambda t: (back(t), 0)),
        out_shape=jax.ShapeDtypeStruct((n, D_MODEL), jnp.float32),
        scratch_shapes=[pltpu.VMEM((D_MODEL, tm), jnp.bfloat16)] + _carry_scratch(tm, d_ff),
        compiler_params=_compiler_params(),
        name=f"attn_layer_{layer}",
    )(sinks[j], h, p, gains, w_qt, w_ot, k, k, v_t, v_t, q_tab_t, *ffn_w)


def _rope_table_t(seq, scale):
    inv = 1.0 / (ROPE_THETA ** (jnp.arange(0, HEAD_DIM, 2, dtype=jnp.float32) / HEAD_DIM))
    ang = jnp.arange(seq, dtype=jnp.float32)[:, None] * inv[None, :]
    return jnp.concatenate([jnp.cos(ang).T, jnp.sin(ang).T], axis=0) * scale


def kernel(x, p, mix_pre_g, mix_post_g, ffn_pre_g, ffn_post_g, pool_w, pool_scale, kv_norm_g, w_k, w_v, w_q, w_o, sinks, w_ff_gate, w_ff_up, w_ff_down, ple_norm_g, w_ple_gate, w_ple_proj):
    b, s, d = x.shape
    n = b * s
    d_ff = w_ff_gate.shape[-1]
    assert d == D_MODEL and s % ROW_TILE == 0 and ROW_TILE % WINDOW == 0 and d_ff % FF_CHUNK == 0
    depth = p.shape[0]
    n_a = pool_w.shape[0]

    pad_scale = jnp.concatenate([pool_scale, jnp.ones((depth - n_a, d), jnp.float32)], axis=0)
    zeros = jnp.zeros((depth, d), jnp.float32)
    gains = jnp.stack([mix_pre_g, mix_post_g, ffn_pre_g, ffn_post_g, ple_norm_g, pad_scale,
                       jnp.broadcast_to(kv_norm_g, (depth, d)), zeros], axis=1)

    ffn_w = tuple(w.astype(jnp.bfloat16)
                  for w in (w_ff_gate, w_ff_up, w_ff_down, w_ple_gate, w_ple_proj))
    pool_w16 = pool_w.astype(jnp.bfloat16)
    w_kvt16 = jnp.concatenate([w_k, w_v], axis=1).astype(jnp.bfloat16).T
    w_qt16 = jnp.swapaxes(w_q.astype(jnp.bfloat16), 1, 2)
    w_ot16 = jnp.swapaxes(w_o.astype(jnp.bfloat16), 1, 2)
    k_tab_t = _rope_table_t(s, 1.0)
    q_tab_t = _rope_table_t(s, HEAD_DIM ** -0.5 * LOG2_E)

    h = x.reshape(n, d)
    pf = p.reshape(depth, n, p.shape[-1])
    k = v_t = None
    for i in range(depth):
        if i < n_a:
            if i == n_a - 1:
                h, k, v_t = _pool_layer(i, h, pf, gains, pool_w16, ffn_w, (w_kvt16, k_tab_t), seq=s)
            else:
                (h,) = _pool_layer(i, h, pf, gains, pool_w16, ffn_w, seq=s)
        else:
            h = _attn_layer(i, h, pf, gains, sinks, w_qt16, w_ot16, k, v_t, q_tab_t, ffn_w, seq=s)
    return h.reshape(b, s, d)
```
